```python
import math
import jax
import jax.numpy as jnp
from jax import lax
import numpy as np

D_MODEL = 1024
BATCH = 16
SEQ = 2048
DEPTH = 2

A_HEADS = 4
A_HEAD_DIM = D_MODEL // A_HEADS
A_WIDTH = A_HEADS * A_HEAD_DIM
A_CHUNK = 128
B_WIDTH = D_MODEL
B_CONV = 31
AB_IN = 4 * A_WIDTH + 2 * A_HEADS + 2 * B_WIDTH
AB_OUT = A_WIDTH + B_WIDTH
C_PATTERNS = ((128, 1), (512, 4), (2048, 16))
C_HEADS = 4
C_HEAD_DIM = 128
C_BLOCK = 128
ROPE_THETA = 10000.0
C_IN = len(C_PATTERNS) * 3 * C_HEADS * C_HEAD_DIM
C_OUT = C_HEADS * C_HEAD_DIM
D_FF = 2816
FFN_CONV = 3
EPS = 1e-6
N_EVEN = (DEPTH + 1) // 2
N_ODD = DEPTH // 2

kernel_name = "hybrid_mlstm_conformer_dilated_attn_trunk"


def _rms_norm(x, g):
    x32 = x.astype(jnp.float32)
    y = x32 * lax.rsqrt(jnp.mean(jnp.square(x32), axis=-1, keepdims=True) + EPS)
    return (y * g.astype(jnp.float32)).astype(x.dtype)


def _layer_norm(x, g, b):
    x32 = x.astype(jnp.float32)
    mu = jnp.mean(x32, axis=-1, keepdims=True)
    var = jnp.mean(jnp.square(x32 - mu), axis=-1, keepdims=True)
    y = (x32 - mu) * lax.rsqrt(var + EPS) * g.astype(jnp.float32) + b.astype(jnp.float32)
    return y.astype(x.dtype)


def _causal_dwconv(x, w, b):
    k_width = w.shape[0]
    y = lax.conv_general_dilated(
        x, w[:, None, :].astype(x.dtype), window_strides=(1,), padding=[(k_width - 1, 0)],
        dimension_numbers=("NWC", "WIO", "NWC"), feature_group_count=x.shape[-1])
    return y + b.astype(x.dtype)


def _rope(x, cos, sin):
    x32 = x.astype(jnp.float32)
    x1, x2 = jnp.split(x32, 2, axis=-1)
    c = cos[:, None, :]
    s = sin[:, None, :]
    return jnp.concatenate([x1 * c - x2 * s, x1 * s + x2 * c], axis=-1).astype(x.dtype)


def _mlstm_chunkwise(q, k, v, i_pre, f_pre):
    bsz, seq, nh, hd = q.shape
    nc = seq // A_CHUNK

    def to_chunks(t):
        return t.astype(jnp.float32).reshape(bsz, nc, A_CHUNK, nh, hd).transpose(1, 0, 3, 2, 4)

    def gate_chunks(t):
        return t.reshape(bsz, nc, A_CHUNK, nh).transpose(1, 0, 3, 2)

    qc = to_chunks(q)
    kc = to_chunks(k) * (hd ** -0.5)
    vc = to_chunks(v)
    log_f = gate_chunks(jax.nn.log_sigmoid(f_pre))
    b_cum = jnp.cumsum(log_f, axis=-1)
    i_log = gate_chunks(i_pre)
    causal = jnp.tril(jnp.ones((A_CHUNK, A_CHUNK), dtype=bool))

    def step(carry, xs):
        c_st, n_st, m_st = carry
        qx, kx, vx, bx, ix = xs
        d_log = jnp.where(causal, bx[..., :, None] - bx[..., None, :] + ix[..., None, :], -jnp.inf)
        inter = bx + m_st[..., None]
        m_t = jnp.maximum(inter, jnp.max(d_log, axis=-1))
        w_intra = jnp.exp(d_log - m_t[..., None]) * jnp.einsum("bhtd,bhsd->bhts", qx, kx)
        e_inter = jnp.exp(inter - m_t)
        num = e_inter[..., None] * jnp.einsum("bhvd,bhtd->bhtv", c_st, qx) + jnp.einsum("bhts,bhsv->bhtv", w_intra, vx)
        den = e_inter * jnp.einsum("bhd,bhtd->bht", n_st, qx) + jnp.sum(w_intra, axis=-1)
        h = num / jnp.maximum(jnp.abs(den), jnp.exp(-m_t))[..., None]
        b_last = bx[..., -1]
        w_log = b_last[..., None] - bx + ix
        m_new = jnp.maximum(b_last + m_st, jnp.max(w_log, axis=-1))
        decay = jnp.exp(b_last + m_st - m_new)
        w_state = jnp.exp(w_log - m_new[..., None])
        c_new = decay[..., None, None] * c_st + jnp.einsum("bhsv,bhsd->bhvd", w_state[..., None] * vx, kx)
        n_new = decay[..., None] * n_st + jnp.einsum("bhs,bhsd->bhd", w_state, kx)
        return (c_new, n_new, m_new), h

    init = (jnp.zeros((bsz, nh, hd, hd), jnp.float32),
            jnp.zeros((bsz, nh, hd), jnp.float32),
            jnp.zeros((bsz, nh), jnp.float32))
    _, h = lax.scan(step, init, (qc, kc, vc, b_cum, i_log))
    return h.transpose(1, 0, 3, 2, 4).reshape(bsz, seq, nh, hd)


def _mlstm_conv_mixer(h, w_in, i_bias, f_bias, head_norm, conv_w, conv_b, ln_g, ln_b, w_out):
    bsz, seq, _ = h.shape
    proj = h @ w_in
    cuts = [A_WIDTH, 2 * A_WIDTH, 3 * A_WIDTH, 4 * A_WIDTH, 4 * A_WIDTH + A_HEADS,
            4 * A_WIDTH + 2 * A_HEADS, 4 * A_WIDTH + 2 * A_HEADS + B_WIDTH]
    q, k, v, o, ig, fg, glu_a, glu_g = jnp.split(proj, cuts, axis=-1)
    heads = lambda t: t.reshape(bsz, seq, A_HEADS, A_HEAD_DIM)
    ig = ig.astype(jnp.float32) + i_bias.astype(jnp.float32)
    fg = fg.astype(jnp.float32) + f_bias.astype(jnp.float32)
    hm = _mlstm_chunkwise(heads(q), heads(k), heads(v), ig, fg)
    hm = hm * lax.rsqrt(jnp.mean(jnp.square(hm), axis=-1, keepdims=True) + EPS)
    hm = hm * head_norm.astype(jnp.float32).reshape(A_HEADS, A_HEAD_DIM)
    hm = (hm.reshape(bsz, seq, A_WIDTH) * jax.nn.sigmoid(o.astype(jnp.float32))).astype(h.dtype)
    u = glu_a * jax.nn.sigmoid(glu_g)
    u = _causal_dwconv(u, conv_w, conv_b)
    u = jax.nn.silu(_layer_norm(u, ln_g, ln_b))
    return jnp.concatenate([hm, u], axis=-1) @ w_out


def _fold(t, d):
    bsz, seq = t.shape[:2]
    rest = t.shape[2:]
    t = jnp.moveaxis(t.reshape((bsz, seq // d, d) + rest), 2, 1)
    return t.reshape((bsz * d, seq // d) + rest)


def _unfold(t, bsz, d):
    n, length = t.shape[:2]
    rest = t.shape[2:]
    t = jnp.moveaxis(t.reshape((bsz, d, length) + rest), 1, 2)
    return t.reshape((bsz, length * d) + rest)


def _window_attention(q, k, v, span):
    n, length, nh, hd = q.shape
    bq = math.gcd(length, C_BLOCK)
    nb = length // bq
    kw = bq + span
    pad = ((0, 0), (span, 0), (0, 0), (0, 0))
    idx = jnp.arange(nb)[:, None] * bq + jnp.arange(kw)[None, :]
    kb = jnp.pad(k, pad)[:, idx]
    vb = jnp.pad(v, pad)[:, idx]
    qb = q.reshape(n, nb, bq, nh, hd)
    s = jnp.einsum("nbqhd,nbkhd->nbhqk", qb, kb).astype(jnp.float32)
    qi = jnp.arange(bq)[None, :, None]
    kj = jnp.arange(kw)[None, None, :]
    blk = jnp.arange(nb)[:, None, None]
    dist = qi - kj + span
    valid = (dist >= 0) & (dist <= span) & (blk * bq + kj >= span)
    s = jnp.where(valid[None, :, None], s, -jnp.inf)
    m = jnp.max(s, axis=-1)
    p = jnp.exp(s - m[..., None])
    den = jnp.sum(p, axis=-1)
    num = jnp.einsum("nbhqk,nbkhd->nbqhd", p, vb.astype(jnp.float32))
    m = m.transpose(0, 1, 3, 2).reshape(n, length, nh)
    den = den.transpose(0, 1, 3, 2).reshape(n, length, nh)
    return num.reshape(n, length, nh, hd), den, m


def _dilated_attention(h, w_qkv, w_out):
    bsz, seq, _ = h.shape
    proj = (h @ w_qkv).reshape(bsz, seq, len(C_PATTERNS), 3, C_HEADS, C_HEAD_DIM)
    pos = jnp.arange(seq, dtype=jnp.float32)
    inv_freq = ROPE_THETA ** (-jnp.arange(0, C_HEAD_DIM, 2, dtype=jnp.float32) / C_HEAD_DIM)
    ang = pos[:, None] * inv_freq[None, :]
    cos, sin = jnp.cos(ang), jnp.sin(ang)
    nums, dens, maxs = [], [], []
    for g, (window, dil) in enumerate(C_PATTERNS):
        q = _rope(proj[:, :, g, 0], cos, sin) * (C_HEAD_DIM ** -0.5)
        k = _rope(proj[:, :, g, 1], cos, sin)
        v = proj[:, :, g, 2]
        num, den, m = _window_attention(_fold(q, dil), _fold(k, dil), _fold(v, dil), window // dil)
        nums.append(_unfold(num, bsz, dil))
        dens.append(_unfold(den, bsz, dil))
        maxs.append(_unfold(m, bsz, dil))
    m_all = jnp.stack(maxs)
    wts = jnp.exp(m_all - jnp.max(m_all, axis=0, keepdims=True))
    o = jnp.sum(wts[..., None] * jnp.stack(nums), axis=0) / jnp.sum(wts * jnp.stack(dens), axis=0)[..., None]
    return o.reshape(bsz, seq, C_OUT).astype(h.dtype) @ w_out


def _conv_ffn(h, w_gu, conv_w, conv_b, w_down):
    gate, up = jnp.split(h @ w_gu, 2, axis=-1)
    gate = _causal_dwconv(gate, conv_w, conv_b)
    return (jax.nn.silu(gate) * up) @ w_down


def setup_inputs(seed: int = 0) -> dict:
    key = jax.random.key(seed)
    ks = jax.random.split(key, 24)
    nrm = lambda k, shape, scale: jax.random.normal(k, shape, jnp.float32) * scale
    f_bias = jnp.linspace(3.0, 6.0, A_HEADS, dtype=jnp.float32)[None, :] + nrm(ks[4], (N_EVEN, A_HEADS), 0.1)
    return {
        "x": nrm(ks[0], (BATCH, SEQ, D_MODEL), 1.0),
        "mix_norm": 1.0 + nrm(ks[1], (DEPTH, D_MODEL), 0.02),
        "ffn_norm": 1.0 + nrm(ks[2], (DEPTH, D_MODEL), 0.02),
        "ab_w_in": nrm(ks[3], (N_EVEN, D_MODEL, AB_IN), D_MODEL ** -0.5),
        "ab_i_bias": nrm(ks[5], (N_EVEN, A_HEADS), 0.1),
        "ab_f_bias": f_bias,
        "ab_head_norm": 1.0 + nrm(ks[6], (N_EVEN, A_WIDTH), 0.02),
        "ab_conv_w": nrm(ks[7], (N_EVEN, B_CONV, B_WIDTH), B_CONV ** -0.5),
        "ab_conv_b": nrm(ks[8], (N_EVEN, B_WIDTH), 0.02),
        "ab_ln_g": 1.0 + nrm(ks[9], (N_EVEN, B_WIDTH), 0.02),
        "ab_ln_b": nrm(ks[10], (N_EVEN, B_WIDTH), 0.02),
        "ab_w_out": nrm(ks[11], (N_EVEN, AB_OUT, D_MODEL), AB_OUT ** -0.5),
        "c_w_qkv": nrm(ks[12], (N_ODD, D_MODEL, C_IN), D_MODEL ** -0.5),
        "c_w_out": nrm(ks[13], (N_ODD, C_OUT, D_MODEL), C_OUT ** -0.5),
        "ffn_w_gu": nrm(ks[14], (DEPTH, D_MODEL, 2 * D_FF), D_MODEL ** -0.5),
        "ffn_conv_w": nrm(ks[15], (DEPTH, FFN_CONV, D_FF), FFN_CONV ** -0.5),
        "ffn_conv_b": nrm(ks[16], (DEPTH, D_FF), 0.02),
        "ffn_w_down": nrm(ks[17], (DEPTH, D_FF, D_MODEL), D_FF ** -0.5),
        "final_norm": 1.0 + nrm(ks[18], (D_MODEL,), 0.02),
    }


def reference(x, mix_norm, ffn_norm, ab_w_in, ab_i_bias, ab_f_bias, ab_head_norm, ab_conv_w, ab_conv_b,
              ab_ln_g, ab_ln_b, ab_w_out, c_w_qkv, c_w_out, ffn_w_gu, ffn_conv_w, ffn_conv_b, ffn_w_down,
              final_norm):
    for layer in range(DEPTH):
        j = layer // 2
        hn = _rms_norm(x, mix_norm[layer])
        if layer % 2 == 0:
            mix = _mlstm_conv_mixer(hn, ab_w_in[j], ab_i_bias[j], ab_f_bias[j], ab_head_norm[j], ab_conv_w[j],
                                    ab_conv_b[j], ab_ln_g[j], ab_ln_b[j], ab_w_out[j])
        else:
            mix = _dilated_attention(hn, c_w_qkv[j], c_w_out[j])
        x = x + mix.astype(x.dtype)
        hn = _rms_norm(x, ffn_norm[layer])
        x = x + _conv_ffn(hn, ffn_w_gu[layer], ffn_conv_w[layer], ffn_conv_b[layer], ffn_w_down[layer]).astype(x.dtype)
    return _rms_norm(x, final_norm)
```

```python
import functools
import math

import jax
import jax.numpy as jnp
from jax import lax
from jax.experimental import pallas as pl
from jax.experimental.pallas import tpu as pltpu

F32 = jnp.float32
BF16 = jnp.bfloat16

EPS = 1e-6
D_MODEL = 1024
A_HEADS = 4
A_HEAD_DIM = 256
A_WIDTH = A_HEADS * A_HEAD_DIM
B_WIDTH = 1024
B_CONV = 31
C_PATTERNS = ((128, 1), (512, 4), (2048, 16))
C_HEADS = 4
C_HEAD_DIM = 128
C_OUT = C_HEADS * C_HEAD_DIM
C_SPAN = 128
ROPE_THETA = 10000.0
D_FF = 2816
FFN_CONV = 3

LANES = 128
SUBLANES = 8
VMEM_LIMIT = 56 * 1024 * 1024

TOKEN_TILE = 512
MLSTM_CHUNK = 256
CONV_TILE = 256
CONV_ROWS = 32
CONV_HALO = 32
FFN_CHUNK = 256
PROJ_CHUNK = 512
ATT_BLOCK = 128


def _sigmoid(x):
    return 1.0 / (1.0 + jnp.exp(-x))


def _log_sigmoid(x):
    return jnp.minimum(x, 0.0) - jnp.log1p(jnp.exp(-jnp.abs(x)))


def _rms_norm_rows(x, g):
    ms = jnp.mean(x * x, axis=-1, keepdims=True)
    return x * lax.rsqrt(ms + EPS) * g


def _dot(a, b):
    return jnp.dot(a, b, preferred_element_type=F32)


def _resident(shape):
    nd = len(shape)
    return pl.BlockSpec(shape, lambda *_: (0,) * nd, pipeline_mode=pl.Buffered(1))


def _params(n_axes):
    return pltpu.CompilerParams(dimension_semantics=("arbitrary",) * n_axes, vmem_limit_bytes=VMEM_LIMIT)


def _inproj_kernel(x_ref, g_ref, wqkvo_ref, wgate_ref, bgate_ref, wglu_ref, qkvo_ref, gates_ref, u_ref):
    hn = _rms_norm_rows(x_ref[...], g_ref[...]).astype(BF16)
    for c in range(4 * A_WIDTH // PROJ_CHUNK):
        cols = slice(c * PROJ_CHUNK, (c + 1) * PROJ_CHUNK)
        qkvo_ref[:, cols] = _dot(hn, wqkvo_ref[:, cols]).astype(BF16)
    gates_ref[...] = _dot(hn, wgate_ref[...]) + bgate_ref[...]
    for c in range(B_WIDTH // PROJ_CHUNK):
        cols = slice(c * PROJ_CHUNK, (c + 1) * PROJ_CHUNK)
        gcols = slice(B_WIDTH + c * PROJ_CHUNK, B_WIDTH + (c + 1) * PROJ_CHUNK)
        u_ref[:, cols] = _dot(hn, wglu_ref[:, cols]) * _sigmoid(_dot(hn, wglu_ref[:, gcols]))


def _inproj(x2d, gain, w_qkvo, w_gate, b_gate, w_glu):
    t = x2d.shape[0]
    tm = TOKEN_TILE
    return pl.pallas_call(
        _inproj_kernel,
        grid=(t // tm,),
        in_specs=[
            pl.BlockSpec((tm, D_MODEL), lambda i: (i, 0)),
            _resident((1, D_MODEL)),
            _resident(w_qkvo.shape),
            _resident(w_gate.shape),
            _resident(b_gate.shape),
            _resident(w_glu.shape),
        ],
        out_specs=[
            pl.BlockSpec((tm, 4 * A_WIDTH), lambda i: (i, 0)),
            pl.BlockSpec((tm, LANES), lambda i: (i, 0)),
            pl.BlockSpec((tm, B_WIDTH), lambda i: (i, 0)),
        ],
        out_shape=[
            jax.ShapeDtypeStruct((t, 4 * A_WIDTH), BF16),
            jax.ShapeDtypeStruct((t, LANES), F32),
            jax.ShapeDtypeStruct((t, B_WIDTH), F32),
        ],
        compiler_params=_params(1),
        name="inproj",
    )(x2d, gain, w_qkvo, w_gate, b_gate, w_glu)


def _cumsum_rows(x):
    n = x.shape[0]
    row = lax.broadcasted_iota(jnp.int32, x.shape, 0)
    s = 1
    while s < n:
        x = x + jnp.where(row >= s, pltpu.roll(x, s, axis=0), 0.0)
        s *= 2
    return x


def _mlstm_kernel(q_ref, k_ref, v_ref, o_ref, gates_ref, hnorm_ref, out_ref, ct_ref, n_ref, m_ref):
    chunk = q_ref.shape[0]
    kscale = A_HEAD_DIM ** -0.5

    @pl.when(pl.program_id(1) == 0)
    def _():
        ct_ref[...] = jnp.zeros_like(ct_ref)
        n_ref[...] = jnp.zeros_like(n_ref)
        m_ref[...] = jnp.zeros_like(m_ref)

    g = gates_ref[...]
    bcum = _cumsum_rows(_log_sigmoid(g))
    lane = lax.broadcasted_iota(jnp.int32, g.shape, 1)
    rows_t = jnp.where(lane < A_HEADS, g, bcum).T
    ti = lax.broadcasted_iota(jnp.int32, (chunk, chunk), 0)
    si = lax.broadcasted_iota(jnp.int32, (chunk, chunk), 1)
    causal = si <= ti

    for h in range(A_HEADS):
        cols = slice(h * A_HEAD_DIM, (h + 1) * A_HEAD_DIM)
        q = q_ref[:, cols]
        k = k_ref[:, cols]
        v = v_ref[:, cols]
        b_col = bcum[:, A_HEADS + h:A_HEADS + h + 1]
        i_col = g[:, h:h + 1]
        i_row = rows_t[h:h + 1, :]
        b_row = rows_t[A_HEADS + h:A_HEADS + h + 1, :]
        m_prev = m_ref[h][:, 0:1]
        ct = ct_ref[h]
        n_row = n_ref[h]

        d_log = jnp.where(causal, b_col + (i_row - b_row), -jnp.inf)
        inter = b_col + m_prev
        m_t = jnp.maximum(inter, jnp.max(d_log, axis=1, keepdims=True))
        s = lax.dot_general(q, k, (((1,), (1,)), ((), ())), preferred_element_type=F32)
        w = jnp.exp(d_log - m_t) * (s * kscale)
        e_inter = jnp.exp(inter - m_t)
        num = e_inter * _dot(q, ct.astype(BF16)) + _dot(w.astype(BF16), v)
        nq = jnp.sum(q.astype(F32) * n_row, axis=1, keepdims=True)
        den = e_inter * nq + jnp.sum(w, axis=1, keepdims=True)
        hh = num / jnp.maximum(jnp.abs(den), jnp.exp(-m_t))

        b_last = b_col[chunk - 1:chunk, :]
        w_log = b_last - b_col + i_col
        m_new = jnp.maximum(b_last + m_prev, jnp.max(w_log, axis=0, keepdims=True))
        decay = jnp.exp(b_last + m_prev - m_new)
        w_state = jnp.exp(w_log - m_new) * kscale
        wv = (w_state * v.astype(F32)).astype(BF16)
        ct_ref[h] = decay * ct + lax.dot_general(k, wv, (((0,), (0,)), ((), ())), preferred_element_type=F32)
        n_ref[h] = decay * n_row + jnp.sum(w_state * k.astype(F32), axis=0, keepdims=True)
        m_ref[h] = jnp.broadcast_to(m_new, (1, LANES))

        hh = hh * lax.rsqrt(jnp.mean(hh * hh, axis=-1, keepdims=True) + EPS) * hnorm_ref[:, cols]
        out_ref[:, cols] = (hh * _sigmoid(o_ref[:, cols].astype(F32))).astype(BF16)


def _mlstm(qkvo, gates, head_norm, bsz, seq):
    chunk = MLSTM_CHUNK
    nc = seq // chunk
    row = lambda b, c: b * nc + c
    col_spec = lambda j: pl.BlockSpec((chunk, A_WIDTH), lambda b, c: (row(b, c), j))
    return pl.pallas_call(
        _mlstm_kernel,
        grid=(bsz, nc),
        in_specs=[col_spec(0), col_spec(1), col_spec(2), col_spec(3),
                  pl.BlockSpec((chunk, LANES), lambda b, c: (row(b, c), 0)),
                  _resident((1, A_WIDTH))],
        out_specs=pl.BlockSpec((chunk, A_WIDTH), lambda b, c: (row(b, c), 0)),
        out_shape=jax.ShapeDtypeStruct((bsz * seq, A_WIDTH), BF16),
        scratch_shapes=[
            pltpu.VMEM((A_HEADS, A_HEAD_DIM, A_HEAD_DIM), F32),
            pltpu.VMEM((A_HEADS, 1, A_HEAD_DIM), F32),
            pltpu.VMEM((A_HEADS, 1, LANES), F32),
        ],
        compiler_params=_params(2),
        name="mlstm",
    )(qkvo, qkvo, qkvo, qkvo, gates, head_norm)


def _conv_kernel(u_ref, halo_ref, w_ref, b_ref, lng_ref, lnb_ref, out_ref, buf_ref):
    ts = u_ref.shape[0]

    @pl.when(pl.program_id(1) == 0)
    def _():
        buf_ref[0:CONV_HALO, :] = jnp.zeros((CONV_HALO, B_WIDTH), F32)

    @pl.when(pl.program_id(1) != 0)
    def _():
        buf_ref[0:CONV_HALO, :] = halo_ref[...]

    buf_ref[CONV_HALO:CONV_HALO + ts, :] = u_ref[...]
    first = CONV_HALO - (B_CONV - 1)

    for r in range(ts // CONV_ROWS):
        base = r * CONV_ROWS
        acc = jnp.zeros((CONV_ROWS, B_WIDTH), F32) + b_ref[...]
        for tap in range(B_CONV):
            acc = acc + buf_ref[base + first + tap:base + first + tap + CONV_ROWS, :] * w_ref[tap:tap + 1, :]
        mu = jnp.mean(acc, axis=-1, keepdims=True)
        cen = acc - mu
        var = jnp.mean(cen * cen, axis=-1, keepdims=True)
        y = cen * lax.rsqrt(var + EPS) * lng_ref[...] + lnb_ref[...]
        out_ref[base:base + CONV_ROWS, :] = (y * _sigmoid(y)).astype(BF16)


def _conv_module(u, conv_w, conv_b, ln_g, ln_b, bsz, seq):
    ts = CONV_TILE
    nt = seq // ts
    halo_blocks = ts // CONV_HALO
    return pl.pallas_call(
        _conv_kernel,
        grid=(bsz, nt),
        in_specs=[
            pl.BlockSpec((ts, B_WIDTH), lambda b, i: (b * nt + i, 0)),
            pl.BlockSpec((CONV_HALO, B_WIDTH), lambda b, i: (jnp.maximum((b * nt + i) * halo_blocks - 1, 0), 0)),
            _resident(conv_w.shape),
            _resident((1, B_WIDTH)),
            _resident((1, B_WIDTH)),
            _resident((1, B_WIDTH)),
        ],
        out_specs=pl.BlockSpec((ts, B_WIDTH), lambda b, i: (b * nt + i, 0)),
        out_shape=jax.ShapeDtypeStruct((bsz * seq, B_WIDTH), BF16),
        scratch_shapes=[pltpu.VMEM((CONV_HALO + ts, B_WIDTH), F32)],
        compiler_params=_params(2),
        name="conv_module",
    )(u, u, conv_w, conv_b, ln_g, ln_b)


def _out_ffn_kernel(*refs, n_mix, final):
    x_ref = refs[0]
    mix_refs = refs[1:1 + n_mix]
    wout_refs = refs[1 + n_mix:1 + 2 * n_mix]
    (g_ref, wgu_ref, cw_ref, cb_ref, wdown_ref, fin_ref, out_ref, x1_ref, acc_ref, gbuf_ref, carry_ref) = refs[1 + 2 * n_mix:]
    tm = x_ref.shape[0]

    @pl.when(pl.program_id(1) == 0)
    def _():
        carry_ref[...] = jnp.zeros_like(carry_ref)

    x1 = x_ref[...]
    for a_ref, w_ref in zip(mix_refs, wout_refs):
        x1 = x1 + _dot(a_ref[...], w_ref[...])
    x1_ref[...] = x1
    hn = _rms_norm_rows(x1, g_ref[...]).astype(BF16)

    for c in range(D_FF // FFN_CHUNK):
        cols = slice(c * FFN_CHUNK, (c + 1) * FFN_CHUNK)
        ucols = slice(D_FF + c * FFN_CHUNK, D_FF + (c + 1) * FFN_CHUNK)
        gate = _dot(hn, wgu_ref[:, cols])
        up = _dot(hn, wgu_ref[:, ucols])
        gbuf_ref[0:SUBLANES, :] = carry_ref[c]
        gbuf_ref[SUBLANES:SUBLANES + tm, :] = gate
        carry_ref[c] = gate[tm - SUBLANES:tm, :]
        y = gate * cw_ref[2:3, cols] + cb_ref[:, cols]
        y = y + gbuf_ref[SUBLANES - 1:SUBLANES - 1 + tm, :] * cw_ref[1:2, cols]
        y = y + gbuf_ref[SUBLANES - 2:SUBLANES - 2 + tm, :] * cw_ref[0:1, cols]
        hid = (y * _sigmoid(y) * up).astype(BF16)
        contrib = _dot(hid, wdown_ref[cols, :])
        if c == 0:
            acc_ref[...] = contrib
        else:
            acc_ref[...] += contrib

    y = x1_ref[...] + acc_ref[...]
    if final:
        y = _rms_norm_rows(y, fin_ref[...])
    out_ref[...] = y


def _out_ffn(x2d, mixes, w_outs, gain, w_gu, conv_w, conv_b, w_down, final_gain, bsz, seq, final):
    tm = TOKEN_TILE
    nt = seq // tm
    n_mix = len(mixes)
    row_spec = lambda width: pl.BlockSpec((tm, width), lambda b, i: (b * nt + i, 0))
    kernel = functools.partial(_out_ffn_kernel, n_mix=n_mix, final=final)
    return pl.pallas_call(
        kernel,
        grid=(bsz, nt),
        in_specs=[row_spec(D_MODEL)] + [row_spec(m.shape[1]) for m in mixes] + [_resident(w.shape) for w in w_outs] + [
            _resident((1, D_MODEL)),
            _resident(w_gu.shape),
            _resident(conv_w.shape),
            _resident((1, D_FF)),
            _resident(w_down.shape),
            _resident((1, D_MODEL)),
        ],
        out_specs=row_spec(D_MODEL),
        out_shape=jax.ShapeDtypeStruct((bsz * seq, D_MODEL), F32),
        scratch_shapes=[
            pltpu.VMEM((tm, D_MODEL), F32),
            pltpu.VMEM((tm, D_MODEL), F32),
            pltpu.VMEM((SUBLANES + tm, FFN_CHUNK), F32),
            pltpu.VMEM((D_FF // FFN_CHUNK, SUBLANES, FFN_CHUNK), F32),
        ],
        compiler_params=_params(2),
        name="out_ffn_final" if final else "out_ffn",
    )(x2d, *mixes, *w_outs, gain, w_gu, conv_w, conv_b, w_down, final_gain)


def _qkv_kernel(x_ref, g_ref, w_ref, cos_ref, sin_ref, *rest):
    out_refs = rest[:9]
    fold_ref = rest[9]
    tm = x_ref.shape[0]
    hn = _rms_norm_rows(x_ref[...], g_ref[...]).astype(BF16)
    cos = cos_ref[...]
    sin = sin_ref[...]
    qscale = C_HEAD_DIM ** -0.5
    for gi, (_, dil) in enumerate(C_PATTERNS):
        for j in range(3):
            c0 = (gi * 3 + j) * C_OUT
            y = _dot(hn, w_ref[:, c0:c0 + C_OUT])
            if j < 2:
                parts = []
                for h in range(C_HEADS):
                    yh = y[:, h * C_HEAD_DIM:(h + 1) * C_HEAD_DIM]
                    yh = yh * cos + pltpu.roll(yh, C_HEAD_DIM // 2, axis=1) * sin
                    parts.append(yh * qscale if j == 0 else yh)
                y = jnp.concatenate(parts, axis=1)
            o_ref = out_refs[gi * 3 + j]
            if dil == 1:
                o_ref[0, 0] = y.astype(BF16)
            else:
                for h in range(C_HEADS):
                    hcols = slice(h * C_HEAD_DIM, (h + 1) * C_HEAD_DIM)
                    fold_ref[h] = y[:, hcols]
                    for r in range(dil):
                        o_ref[0, r, :, hcols] = fold_ref[h, pl.ds(r, tm // dil, stride=dil), :].astype(BF16)


def _rope_tables(seq):
    pos = jnp.arange(seq, dtype=F32)
    inv_freq = ROPE_THETA ** (-jnp.arange(0, C_HEAD_DIM, 2, dtype=F32) / C_HEAD_DIM)
    ang = pos[:, None] * inv_freq[None, :]
    cos, sin = jnp.cos(ang), jnp.sin(ang)
    return jnp.concatenate([cos, cos], axis=1), jnp.concatenate([-sin, sin], axis=1)


def _qkv(x2d, gain, w_qkv, bsz, seq):
    tm = TOKEN_TILE
    nt = seq // tm
    cos, sin = _rope_tables(seq)
    out_specs, out_shapes = [], []
    for _, dil in C_PATTERNS:
        for _ in range(3):
            out_specs.append(pl.BlockSpec((1, dil, tm // dil, C_OUT), lambda b, i: (b, 0, i, 0)))
            out_shapes.append(jax.ShapeDtypeStruct((bsz, dil, seq // dil, C_OUT), BF16))
    return pl.pallas_call(
        _qkv_kernel,
        grid=(bsz, nt),
        in_specs=[
            pl.BlockSpec((tm, D_MODEL), lambda b, i: (b * nt + i, 0)),
            _resident((1, D_MODEL)),
            _resident(w_qkv.shape),
            pl.BlockSpec((tm, C_HEAD_DIM), lambda b, i: (i, 0)),
            pl.BlockSpec((tm, C_HEAD_DIM), lambda b, i: (i, 0)),
        ],
        out_specs=out_specs,
        out_shape=out_shapes,
        scratch_shapes=[pltpu.VMEM((C_HEADS, tm, C_HEAD_DIM), F32)],
        compiler_params=_params(2),
        name="qkv_rope_fold",
    )(x2d, gain, w_qkv, cos, sin)


def _attn_kernel(*refs):
    qkv_refs = refs[:9]
    out_ref = refs[9]
    num_ref, den_ref, max_ref = refs[10:]
    seq = out_ref.shape[1]
    blk = ATT_BLOCK
    qi = lax.broadcasted_iota(jnp.int32, (blk, blk), 0)
    ki = lax.broadcasted_iota(jnp.int32, (blk, blk), 1)
    cur_ok = ki <= qi
    prev_ok = ki >= qi

    for gi, (_, dil) in enumerate(C_PATTERNS):
        q_ref, k_ref, v_ref = qkv_refs[gi * 3:gi * 3 + 3]
        nblk = seq // dil // blk
        for r in range(dil):
            for j in range(nblk):
                rows = slice(j * blk, (j + 1) * blk)
                q = q_ref[0, r, rows, :]
                s_cur = lax.dot_general(q, k_ref[0, r, rows, :], (((1,), (1,)), ((), ())), preferred_element_type=F32)
                s_cur = jnp.where(cur_ok, s_cur, -jnp.inf)
                m = jnp.max(s_cur, axis=1, keepdims=True)
                if j > 0:
                    prow = slice((j - 1) * blk, j * blk)
                    s_prev = lax.dot_general(q, k_ref[0, r, prow, :], (((1,), (1,)), ((), ())), preferred_element_type=F32)
                    s_prev = jnp.where(prev_ok, s_prev, -jnp.inf)
                    m = jnp.maximum(m, jnp.max(s_prev, axis=1, keepdims=True))
                p = jnp.exp(s_cur - m)
                den = jnp.sum(p, axis=1, keepdims=True)
                num = _dot(p.astype(BF16), v_ref[0, r, rows, :])
                if j > 0:
                    p = jnp.exp(s_prev - m)
                    den = den + jnp.sum(p, axis=1, keepdims=True)
                    num = num + _dot(p.astype(BF16), v_ref[0, r, prow, :])
                if dil == 1:
                    dst = pl.ds(j * blk, blk)
                else:
                    dst = pl.ds(r + dil * blk * j, blk, stride=dil)
                num_ref[gi, dst, :] = num
                den_ref[gi, dst, :] = jnp.broadcast_to(den, (blk, LANES))
                max_ref[gi, dst, :] = jnp.broadcast_to(m, (blk, LANES))

    def merge(c, carry):
        rows = pl.ds(pl.multiple_of(c * blk, blk), blk)
        ms = [max_ref[gi, rows, :] for gi in range(len(C_PATTERNS))]
        m_all = functools.reduce(jnp.maximum, ms)
        top = jnp.zeros((blk, C_HEAD_DIM), F32)
        bot = jnp.zeros((blk, C_HEAD_DIM), F32)
        for gi in range(len(C_PATTERNS)):
            wt = jnp.exp(ms[gi] - m_all)
            top = top + wt * num_ref[gi, rows, :]
            bot = bot + wt * den_ref[gi, rows, :]
        out_ref[0, rows, :] = (top / bot).astype(BF16)
        return carry

    lax.fori_loop(0, seq // blk, merge, 0)


def _attention(folded, bsz, seq):
    in_specs = []
    for _, dil in C_PATTERNS:
        for _ in range(3):
            in_specs.append(pl.BlockSpec((1, dil, seq // dil, C_HEAD_DIM), lambda b, h: (b, 0, 0, h)))
    n_groups = len(C_PATTERNS)
    return pl.pallas_call(
        _attn_kernel,
        grid=(bsz, C_HEADS),
        in_specs=in_specs,
        out_specs=pl.BlockSpec((1, seq, C_HEAD_DIM), lambda b, h: (b, 0, h)),
        out_shape=jax.ShapeDtypeStruct((bsz, seq, C_OUT), BF16),
        scratch_shapes=[
            pltpu.VMEM((n_groups, seq, C_HEAD_DIM), F32),
            pltpu.VMEM((n_groups, seq, LANES), F32),
            pltpu.VMEM((n_groups, seq, LANES), F32),
        ],
        compiler_params=_params(2),
        name="dilated_attention",
    )(*folded)


def _layer0(x2d, mix_gain, ffn_gain, w_in, i_bias, f_bias, head_norm, conv_w, conv_b, ln_g, ln_b, w_out,
            w_gu, ffn_cw, ffn_cb, w_down, final_gain, bsz, seq, final):
    gate0 = 4 * A_WIDTH
    glu0 = gate0 + 2 * A_HEADS
    w_qkvo = w_in[:, :gate0].astype(BF16)
    w_gate = jnp.pad(w_in[:, gate0:glu0], ((0, 0), (0, LANES - 2 * A_HEADS))).astype(BF16)
    b_gate = jnp.pad(jnp.concatenate([i_bias, f_bias]), (0, LANES - 2 * A_HEADS)).astype(F32)[None, :]
    w_glu = w_in[:, glu0:].astype(BF16)
    qkvo, gates, u = _inproj(x2d, mix_gain[None, :], w_qkvo, w_gate, b_gate, w_glu)
    hm = _mlstm(qkvo, gates, head_norm[None, :], bsz, seq)
    uc = _conv_module(u, conv_w, conv_b[None, :], ln_g[None, :], ln_b[None, :], bsz, seq)
    w_out = w_out.astype(BF16)
    return _out_ffn(x2d, [hm, uc], [w_out[:A_WIDTH], w_out[A_WIDTH:]], ffn_gain[None, :], w_gu.astype(BF16),
                    ffn_cw, ffn_cb[None, :], w_down.astype(BF16), final_gain[None, :], bsz, seq, final)


def _layer1(x2d, mix_gain, ffn_gain, w_qkv, w_out, w_gu, ffn_cw, ffn_cb, w_down, final_gain, bsz, seq, final):
    folded = _qkv(x2d, mix_gain[None, :], w_qkv.astype(BF16), bsz, seq)
    att = _attention(folded, bsz, seq).reshape(bsz * seq, C_OUT)
    return _out_ffn(x2d, [att], [w_out.astype(BF16)], ffn_gain[None, :], w_gu.astype(BF16),
                    ffn_cw, ffn_cb[None, :], w_down.astype(BF16), final_gain[None, :], bsz, seq, final)


def kernel(x, mix_norm, ffn_norm, ab_w_in, ab_i_bias, ab_f_bias, ab_head_norm, ab_conv_w, ab_conv_b, ab_ln_g, ab_ln_b, ab_w_out, c_w_qkv, c_w_out, ffn_w_gu, ffn_conv_w, ffn_conv_b, ffn_w_down, final_norm):
    bsz, seq, _ = x.shape
    depth = mix_norm.shape[0]
    x2d = x.reshape(bsz * seq, D_MODEL)
    for layer in range(depth):
        j = layer // 2
        final = layer == depth - 1
        ffn_args = (ffn_w_gu[layer], ffn_conv_w[layer], ffn_conv_b[layer], ffn_w_down[layer], final_norm, bsz, seq, final)
        if layer % 2 == 0:
            x2d = _layer0(x2d, mix_norm[layer], ffn_norm[layer], ab_w_in[j], ab_i_bias[j], ab_f_bias[j],
                          ab_head_norm[j], ab_conv_w[j], ab_conv_b[j], ab_ln_g[j], ab_ln_b[j], ab_w_out[j], *ffn_args)
        else:
            x2d = _layer1(x2d, mix_norm[layer], ffn_norm[layer], c_w_qkv[j], c_w_out[j], *ffn_args)
    return x2d.reshape(bsz, seq, D_MODEL)
```

```python
import functools
import math

import jax
import jax.numpy as jnp
from jax import lax
from jax.experimental import pallas as pl
from jax.experimental.pallas import tpu as pltpu

F32 = jnp.float32
BF16 = jnp.bfloat16

EPS = 1e-6
D_MODEL = 1024
A_HEADS = 4
A_HEAD_DIM = 256
A_WIDTH = A_HEADS * A_HEAD_DIM
B_WIDTH = 1024
B_CONV = 31
C_PATTERNS = ((128, 1), (512, 4), (2048, 16))
C_HEADS = 4
C_HEAD_DIM = 128
C_OUT = C_HEADS * C_HEAD_DIM
C_SPAN = 128
ROPE_THETA = 10000.0
D_FF = 2816
FFN_CONV = 3

LANES = 128
SUBLANES = 8
VMEM_LIMIT = 56 * 1024 * 1024

TOKEN_TILE = 512
MLSTM_CHUNK = 256
CONV_TILE = 256
CONV_ROWS = 32
CONV_HALO = 32
FFN_CHUNK = 256
PROJ_CHUNK = 512
ATT_BLOCK = 128
ATT_BATCH = 8


def _sigmoid(x):
    return 1.0 / (1.0 + jnp.exp(-x))


def _log_sigmoid(x):
    return jnp.minimum(x, 0.0) - jnp.log1p(jnp.exp(-jnp.abs(x)))


def _rms_norm_rows(x, g):
    ms = jnp.mean(x * x, axis=-1, keepdims=True)
    return x * lax.rsqrt(ms + EPS) * g


def _dot(a, b):
    return jnp.dot(a, b, preferred_element_type=F32)


def _resident(shape):
    nd = len(shape)
    return pl.BlockSpec(shape, lambda *_: (0,) * nd, pipeline_mode=pl.Buffered(1))


def _params(n_axes):
    return pltpu.CompilerParams(dimension_semantics=("arbitrary",) * n_axes, vmem_limit_bytes=VMEM_LIMIT)


def _inproj_kernel(x_ref, g_ref, wqkvo_ref, wgate_ref, bgate_ref, wglu_ref, qkvo_ref, gates_ref, u_ref):
    hn = _rms_norm_rows(x_ref[...], g_ref[...]).astype(BF16)
    for c in range(4 * A_WIDTH // PROJ_CHUNK):
        cols = slice(c * PROJ_CHUNK, (c + 1) * PROJ_CHUNK)
        qkvo_ref[:, cols] = _dot(hn, wqkvo_ref[:, cols]).astype(BF16)
    gates_ref[...] = _dot(hn, wgate_ref[...]) + bgate_ref[...]
    for c in range(B_WIDTH // PROJ_CHUNK):
        cols = slice(c * PROJ_CHUNK, (c + 1) * PROJ_CHUNK)
        gcols = slice(B_WIDTH + c * PROJ_CHUNK, B_WIDTH + (c + 1) * PROJ_CHUNK)
        u_ref[:, cols] = _dot(hn, wglu_ref[:, cols]) * _sigmoid(_dot(hn, wglu_ref[:, gcols]))


def _inproj(x2d, gain, w_qkvo, w_gate, b_gate, w_glu):
    t = x2d.shape[0]
    tm = TOKEN_TILE
    return pl.pallas_call(
        _inproj_kernel,
        grid=(t // tm,),
        in_specs=[
            pl.BlockSpec((tm, D_MODEL), lambda i: (i, 0)),
            _resident((1, D_MODEL)),
            _resident(w_qkvo.shape),
            _resident(w_gate.shape),
            _resident(b_gate.shape),
            _resident(w_glu.shape),
        ],
        out_specs=[
            pl.BlockSpec((tm, 4 * A_WIDTH), lambda i: (i, 0)),
            pl.BlockSpec((tm, LANES), lambda i: (i, 0)),
            pl.BlockSpec((tm, B_WIDTH), lambda i: (i, 0)),
        ],
        out_shape=[
            jax.ShapeDtypeStruct((t, 4 * A_WIDTH), BF16),
            jax.ShapeDtypeStruct((t, LANES), F32),
            jax.ShapeDtypeStruct((t, B_WIDTH), F32),
        ],
        compiler_params=_params(1),
        name="inproj",
    )(x2d, gain, w_qkvo, w_gate, b_gate, w_glu)


def _cumsum_rows(x):
    n = x.shape[0]
    row = lax.broadcasted_iota(jnp.int32, x.shape, 0)
    s = 1
    while s < n:
        x = x + jnp.where(row >= s, pltpu.roll(x, s, axis=0), 0.0)
        s *= 2
    return x


def _mlstm_kernel(q_ref, k_ref, v_ref, o_ref, gates_ref, hnorm_ref, out_ref, ct_ref, n_ref, m_ref):
    chunk = q_ref.shape[0]
    kscale = A_HEAD_DIM ** -0.5

    @pl.when(pl.program_id(1) == 0)
    def _():
        ct_ref[...] = jnp.zeros_like(ct_ref)
        n_ref[...] = jnp.zeros_like(n_ref)
        m_ref[...] = jnp.zeros_like(m_ref)

    g = gates_ref[...]
    bcum = _cumsum_rows(_log_sigmoid(g))
    lane = lax.broadcasted_iota(jnp.int32, g.shape, 1)
    rows_t = jnp.where(lane < A_HEADS, g, bcum).T
    ti = lax.broadcasted_iota(jnp.int32, (chunk, chunk), 0)
    si = lax.broadcasted_iota(jnp.int32, (chunk, chunk), 1)
    causal = si <= ti

    for h in range(A_HEADS):
        cols = slice(h * A_HEAD_DIM, (h + 1) * A_HEAD_DIM)
        q = q_ref[:, cols]
        k = k_ref[:, cols]
        v = v_ref[:, cols]
        b_col = bcum[:, A_HEADS + h:A_HEADS + h + 1]
        i_col = g[:, h:h + 1]
        i_row = rows_t[h:h + 1, :]
        b_row = rows_t[A_HEADS + h:A_HEADS + h + 1, :]
        m_prev = m_ref[h][:, 0:1]
        ct = ct_ref[h]
        n_row = n_ref[h]

        d_log = jnp.where(causal, b_col + (i_row - b_row), -jnp.inf)
        inter = b_col + m_prev
        m_t = jnp.maximum(inter, jnp.max(d_log, axis=1, keepdims=True))
        s = lax.dot_general(q, k, (((1,), (1,)), ((), ())), preferred_element_type=F32)
        w = jnp.exp(d_log - m_t) * (s * kscale)
        e_inter = jnp.exp(inter - m_t)
        num = e_inter * _dot(q, ct.astype(BF16)) + _dot(w.astype(BF16), v)
        nq = jnp.sum(q.astype(F32) * n_row, axis=1, keepdims=True)
        den = e_inter * nq + jnp.sum(w, axis=1, keepdims=True)
        hh = num / jnp.maximum(jnp.abs(den), jnp.exp(-m_t))

        b_last = b_col[chunk - 1:chunk, :]
        w_log = b_last - b_col + i_col
        m_new = jnp.maximum(b_last + m_prev, jnp.max(w_log, axis=0, keepdims=True))
        decay = jnp.exp(b_last + m_prev - m_new)
        w_state = jnp.exp(w_log - m_new) * kscale
        wv = (w_state * v.astype(F32)).astype(BF16)
        ct_ref[h] = decay * ct + lax.dot_general(k, wv, (((0,), (0,)), ((), ())), preferred_element_type=F32)
        n_ref[h] = decay * n_row + jnp.sum(w_state * k.astype(F32), axis=0, keepdims=True)
        m_ref[h] = jnp.broadcast_to(m_new, (1, LANES))

        hh = hh * lax.rsqrt(jnp.mean(hh * hh, axis=-1, keepdims=True) + EPS) * hnorm_ref[:, cols]
        out_ref[:, cols] = (hh * _sigmoid(o_ref[:, cols].astype(F32))).astype(BF16)


def _mlstm(qkvo, gates, head_norm, bsz, seq):
    chunk = MLSTM_CHUNK
    nc = seq // chunk
    row = lambda b, c: b * nc + c
    col_spec = lambda j: pl.BlockSpec((chunk, A_WIDTH), lambda b, c: (row(b, c), j))
    return pl.pallas_call(
        _mlstm_kernel,
        grid=(bsz, nc),
        in_specs=[col_spec(0), col_spec(1), col_spec(2), col_spec(3),
                  pl.BlockSpec((chunk, LANES), lambda b, c: (row(b, c), 0)),
                  _resident((1, A_WIDTH))],
        out_specs=pl.BlockSpec((chunk, A_WIDTH), lambda b, c: (row(b, c), 0)),
        out_shape=jax.ShapeDtypeStruct((bsz * seq, A_WIDTH), BF16),
        scratch_shapes=[
            pltpu.VMEM((A_HEADS, A_HEAD_DIM, A_HEAD_DIM), F32),
            pltpu.VMEM((A_HEADS, 1, A_HEAD_DIM), F32),
            pltpu.VMEM((A_HEADS, 1, LANES), F32),
        ],
        compiler_params=_params(2),
        name="mlstm",
    )(qkvo, qkvo, qkvo, qkvo, gates, head_norm)


def _conv_kernel(u_ref, halo_ref, w_ref, b_ref, lng_ref, lnb_ref, out_ref, buf_ref, shift_ref):
    ts = u_ref.shape[0]

    @pl.when(pl.program_id(1) == 0)
    def _():
        buf_ref[0:CONV_HALO, :] = jnp.zeros((CONV_HALO, B_WIDTH), F32)

    @pl.when(pl.program_id(1) != 0)
    def _():
        buf_ref[0:CONV_HALO, :] = halo_ref[...]

    buf_ref[CONV_HALO:CONV_HALO + ts, :] = u_ref[...]
    first = CONV_HALO - (B_CONV - 1)
    n_shift = shift_ref.shape[1]
    for r in range(1, SUBLANES):
        shift_ref[r - 1] = buf_ref[r:r + n_shift, :]

    for c in range(ts // CONV_ROWS):
        base = c * CONV_ROWS
        acc = jnp.zeros((CONV_ROWS, B_WIDTH), F32) + b_ref[...]
        for tap in range(B_CONV):
            whole, r = divmod(first + tap, SUBLANES)
            lo = base + whole * SUBLANES
            if r == 0:
                win = buf_ref[lo:lo + CONV_ROWS, :]
            else:
                win = shift_ref[r - 1, lo:lo + CONV_ROWS, :]
            acc = acc + win * w_ref[tap:tap + 1, :]
        mu = jnp.mean(acc, axis=-1, keepdims=True)
        cen = acc - mu
        var = jnp.mean(cen * cen, axis=-1, keepdims=True)
        y = cen * lax.rsqrt(var + EPS) * lng_ref[...] + lnb_ref[...]
        out_ref[base:base + CONV_ROWS, :] = (y * _sigmoid(y)).astype(BF16)


def _conv_module(u, conv_w, conv_b, ln_g, ln_b, bsz, seq):
    ts = CONV_TILE
    nt = seq // ts
    halo_blocks = ts // CONV_HALO
    return pl.pallas_call(
        _conv_kernel,
        grid=(bsz, nt),
        in_specs=[
            pl.BlockSpec((ts, B_WIDTH), lambda b, i: (b * nt + i, 0)),
            pl.BlockSpec((CONV_HALO, B_WIDTH), lambda b, i: (jnp.maximum((b * nt + i) * halo_blocks - 1, 0), 0)),
            _resident(conv_w.shape),
            _resident((1, B_WIDTH)),
            _resident((1, B_WIDTH)),
            _resident((1, B_WIDTH)),
        ],
        out_specs=pl.BlockSpec((ts, B_WIDTH), lambda b, i: (b * nt + i, 0)),
        out_shape=jax.ShapeDtypeStruct((bsz * seq, B_WIDTH), BF16),
        scratch_shapes=[
            pltpu.VMEM((CONV_HALO + ts, B_WIDTH), F32),
            pltpu.VMEM((SUBLANES - 1, ts + CONV_HALO - SUBLANES, B_WIDTH), F32),
        ],
        compiler_params=_params(2),
        name="conv_module",
    )(u, u, conv_w, conv_b, ln_g, ln_b)


def _out_ffn_kernel(*refs, n_mix, final):
    x_ref = refs[0]
    mix_refs = refs[1:1 + n_mix]
    wout_refs = refs[1 + n_mix:1 + 2 * n_mix]
    (g_ref, wgu_ref, cw_ref, cb_ref, wdown_ref, fin_ref, out_ref, x1_ref, acc_ref, gbuf_ref, carry_ref) = refs[1 + 2 * n_mix:]
    tm = x_ref.shape[0]

    @pl.when(pl.program_id(1) == 0)
    def _():
        carry_ref[...] = jnp.zeros_like(carry_ref)

    x1 = x_ref[...]
    for a_ref, w_ref in zip(mix_refs, wout_refs):
        x1 = x1 + _dot(a_ref[...], w_ref[...])
    x1_ref[...] = x1
    hn = _rms_norm_rows(x1, g_ref[...]).astype(BF16)

    for c in range(D_FF // FFN_CHUNK):
        cols = slice(c * FFN_CHUNK, (c + 1) * FFN_CHUNK)
        ucols = slice(D_FF + c * FFN_CHUNK, D_FF + (c + 1) * FFN_CHUNK)
        gate = _dot(hn, wgu_ref[:, cols])
        up = _dot(hn, wgu_ref[:, ucols])
        gbuf_ref[0:SUBLANES, :] = carry_ref[c]
        gbuf_ref[SUBLANES:SUBLANES + tm, :] = gate
        carry_ref[c] = gate[tm - SUBLANES:tm, :]
        y = gate * cw_ref[2:3, cols] + cb_ref[:, cols]
        y = y + gbuf_ref[SUBLANES - 1:SUBLANES - 1 + tm, :] * cw_ref[1:2, cols]
        y = y + gbuf_ref[SUBLANES - 2:SUBLANES - 2 + tm, :] * cw_ref[0:1, cols]
        hid = (y * _sigmoid(y) * up).astype(BF16)
        contrib = _dot(hid, wdown_ref[cols, :])
        if c == 0:
            acc_ref[...] = contrib
        else:
            acc_ref[...] += contrib

    y = x1_ref[...] + acc_ref[...]
    if final:
        y = _rms_norm_rows(y, fin_ref[...])
    out_ref[...] = y


def _out_ffn(x2d, mixes, w_outs, gain, w_gu, conv_w, conv_b, w_down, final_gain, bsz, seq, final):
    tm = TOKEN_TILE
    nt = seq // tm
    n_mix = len(mixes)
    row_spec = lambda width: pl.BlockSpec((tm, width), lambda b, i: (b * nt + i, 0))
    kernel = functools.partial(_out_ffn_kernel, n_mix=n_mix, final=final)
    return pl.pallas_call(
        kernel,
        grid=(bsz, nt),
        in_specs=[row_spec(D_MODEL)] + [row_spec(m.shape[1]) for m in mixes] + [_resident(w.shape) for w in w_outs] + [
            _resident((1, D_MODEL)),
            _resident(w_gu.shape),
            _resident(conv_w.shape),
            _resident((1, D_FF)),
            _resident(w_down.shape),
            _resident((1, D_MODEL)),
        ],
        out_specs=row_spec(D_MODEL),
        out_shape=jax.ShapeDtypeStruct((bsz * seq, D_MODEL), F32),
        scratch_shapes=[
            pltpu.VMEM((tm, D_MODEL), F32),
            pltpu.VMEM((tm, D_MODEL), F32),
            pltpu.VMEM((SUBLANES + tm, FFN_CHUNK), F32),
            pltpu.VMEM((D_FF // FFN_CHUNK, SUBLANES, FFN_CHUNK), F32),
        ],
        compiler_params=_params(2),
        name="out_ffn_final" if final else "out_ffn",
    )(x2d, *mixes, *w_outs, gain, w_gu, conv_w, conv_b, w_down, final_gain)


def _qkv_kernel(x_ref, g_ref, w_ref, cos_ref, sin_ref, *rest):
    out_refs = rest[:9]
    fold_ref = rest[9]
    tm = x_ref.shape[0]
    hn = _rms_norm_rows(x_ref[...], g_ref[...]).astype(BF16)
    cos = cos_ref[...]
    sin = sin_ref[...]
    qscale = C_HEAD_DIM ** -0.5
    for gi, (_, dil) in enumerate(C_PATTERNS):
        for j in range(3):
            c0 = (gi * 3 + j) * C_OUT
            y = _dot(hn, w_ref[:, c0:c0 + C_OUT])
            if j < 2:
                parts = []
                for h in range(C_HEADS):
                    yh = y[:, h * C_HEAD_DIM:(h + 1) * C_HEAD_DIM]
                    yh = yh * cos + pltpu.roll(yh, C_HEAD_DIM // 2, axis=1) * sin
                    parts.append(yh * qscale if j == 0 else yh)
                y = jnp.concatenate(parts, axis=1)
            o_ref = out_refs[gi * 3 + j]
            if dil == 1:
                o_ref[0, 0] = y.astype(BF16)
            else:
                for h in range(C_HEADS):
                    hcols = slice(h * C_HEAD_DIM, (h + 1) * C_HEAD_DIM)
                    fold_ref[h] = y[:, hcols]
                    for r in range(dil):
                        o_ref[0, r, :, hcols] = fold_ref[h, pl.ds(r, tm // dil, stride=dil), :].astype(BF16)


def _rope_tables(seq):
    pos = jnp.arange(seq, dtype=F32)
    inv_freq = ROPE_THETA ** (-jnp.arange(0, C_HEAD_DIM, 2, dtype=F32) / C_HEAD_DIM)
    ang = pos[:, None] * inv_freq[None, :]
    cos, sin = jnp.cos(ang), jnp.sin(ang)
    return jnp.concatenate([cos, cos], axis=1), jnp.concatenate([-sin, sin], axis=1)


def _qkv(x2d, gain, w_qkv, bsz, seq):
    tm = TOKEN_TILE
    nt = seq // tm
    cos, sin = _rope_tables(seq)
    out_specs, out_shapes = [], []
    for _, dil in C_PATTERNS:
        for _ in range(3):
            out_specs.append(pl.BlockSpec((1, dil, tm // dil, C_OUT), lambda b, i: (b, 0, i, 0)))
            out_shapes.append(jax.ShapeDtypeStruct((bsz, dil, seq // dil, C_OUT), BF16))
    return pl.pallas_call(
        _qkv_kernel,
        grid=(bsz, nt),
        in_specs=[
            pl.BlockSpec((tm, D_MODEL), lambda b, i: (b * nt + i, 0)),
            _resident((1, D_MODEL)),
            _resident(w_qkv.shape),
            pl.BlockSpec((tm, C_HEAD_DIM), lambda b, i: (i, 0)),
            pl.BlockSpec((tm, C_HEAD_DIM), lambda b, i: (i, 0)),
        ],
        out_specs=out_specs,
        out_shape=out_shapes,
        scratch_shapes=[pltpu.VMEM((C_HEADS, tm, C_HEAD_DIM), F32)],
        compiler_params=_params(2),
        name="qkv_rope_fold",
    )(x2d, gain, w_qkv, cos, sin)


def _attn_kernel(*refs):
    qkv_refs = refs[:9]
    out_ref = refs[9]
    num_ref, den_ref, max_ref = refs[10:]
    seq = out_ref.shape[1]
    blk = ATT_BLOCK
    qi = lax.broadcasted_iota(jnp.int32, (blk, 2 * blk), 0)
    ki = lax.broadcasted_iota(jnp.int32, (blk, 2 * blk), 1)
    band_ok = (ki >= qi) & (ki - blk <= qi)
    first_ok = band_ok[:, blk:]
    ones = jnp.ones((blk, LANES), BF16)

    for gi, (_, dil) in enumerate(C_PATTERNS):
        q_ref, k_ref, v_ref = qkv_refs[gi * 3:gi * 3 + 3]
        nblk = seq // dil // blk
        units = [(r, j) for r in range(dil) for j in range(nblk)]
        for u0 in range(0, len(units), ATT_BATCH):
            batch = units[u0:u0 + ATT_BATCH]
            key_rows = [slice(max(j - 1, 0) * blk, (j + 1) * blk) for _, j in batch]
            scores = []
            for (r, j), krows in zip(batch, key_rows):
                q = q_ref[0, r, j * blk:(j + 1) * blk, :]
                s = lax.dot_general(q, k_ref[0, r, krows, :], (((1,), (1,)), ((), ())), preferred_element_type=F32)
                scores.append(jnp.where(first_ok if j == 0 else band_ok, s, -jnp.inf))
            maxes = [jnp.max(s, axis=1, keepdims=True) for s in scores]
            probs = [jnp.exp(s - m).astype(BF16) for s, m in zip(scores, maxes)]
            for (r, j), krows, p, m in zip(batch, key_rows, probs, maxes):
                v = v_ref[0, r, krows, :]
                v_ones = jnp.concatenate([v, jnp.concatenate([ones] * (v.shape[0] // blk), axis=0)], axis=1)
                num_den = _dot(p, v_ones)
                if dil == 1:
                    dst = pl.ds(j * blk, blk)
                else:
                    dst = pl.ds(r + dil * blk * j, blk, stride=dil)
                num_ref[gi, dst, :] = num_den[:, :C_HEAD_DIM]
                den_ref[gi, dst, :] = num_den[:, C_HEAD_DIM:]
                max_ref[gi, dst, :] = jnp.broadcast_to(m, (blk, LANES))

    def merge(c, carry):
        rows = pl.ds(pl.multiple_of(c * blk, blk), blk)
        ms = [max_ref[gi, rows, :] for gi in range(len(C_PATTERNS))]
        m_all = functools.reduce(jnp.maximum, ms)
        top = jnp.zeros((blk, C_HEAD_DIM), F32)
        bot = jnp.zeros((blk, C_HEAD_DIM), F32)
        for gi in range(len(C_PATTERNS)):
            wt = jnp.exp(ms[gi] - m_all)
            top = top + wt * num_ref[gi, rows, :]
            bot = bot + wt * den_ref[gi, rows, :]
        out_ref[0, rows, :] = (top / bot).astype(BF16)
        return carry

    lax.fori_loop(0, seq // blk, merge, 0)


def _attention(folded, bsz, seq):
    in_specs = []
    for _, dil in C_PATTERNS:
        for _ in range(3):
            in_specs.append(pl.BlockSpec((1, dil, seq // dil, C_HEAD_DIM), lambda b, h: (b, 0, 0, h)))
    n_groups = len(C_PATTERNS)
    return pl.pallas_call(
        _attn_kernel,
        grid=(bsz, C_HEADS),
        in_specs=in_specs,
        out_specs=pl.BlockSpec((1, seq, C_HEAD_DIM), lambda b, h: (b, 0, h)),
        out_shape=jax.ShapeDtypeStruct((bsz, seq, C_OUT), BF16),
        scratch_shapes=[
            pltpu.VMEM((n_groups, seq, C_HEAD_DIM), F32),
            pltpu.VMEM((n_groups, seq, LANES), F32),
            pltpu.VMEM((n_groups, seq, LANES), F32),
        ],
        compiler_params=_params(2),
        name="dilated_attention",
    )(*folded)


def _layer0(x2d, mix_gain, ffn_gain, w_in, i_bias, f_bias, head_norm, conv_w, conv_b, ln_g, ln_b, w_out,
            w_gu, ffn_cw, ffn_cb, w_down, final_gain, bsz, seq, final):
    gate0 = 4 * A_WIDTH
    glu0 = gate0 + 2 * A_HEADS
    w_qkvo = w_in[:, :gate0].astype(BF16)
    w_gate = jnp.pad(w_in[:, gate0:glu0], ((0, 0), (0, LANES - 2 * A_HEADS))).astype(BF16)
    b_gate = jnp.pad(jnp.concatenate([i_bias, f_bias]), (0, LANES - 2 * A_HEADS)).astype(F32)[None, :]
    w_glu = w_in[:, glu0:].astype(BF16)
    qkvo, gates, u = _inproj(x2d, mix_gain[None, :], w_qkvo, w_gate, b_gate, w_glu)
    hm = _mlstm(qkvo, gates, head_norm[None, :], bsz, seq)
    uc = _conv_module(u, conv_w, conv_b[None, :], ln_g[None, :], ln_b[None, :], bsz, seq)
    w_out = w_out.astype(BF16)
    return _out_ffn(x2d, [hm, uc], [w_out[:A_WIDTH], w_out[A_WIDTH:]], ffn_gain[None, :], w_gu.astype(BF16),
                    ffn_cw, ffn_cb[None, :], w_down.astype(BF16), final_gain[None, :], bsz, seq, final)


def _layer1(x2d, mix_gain, ffn_gain, w_qkv, w_out, w_gu, ffn_cw, ffn_cb, w_down, final_gain, bsz, seq, final):
    folded = _qkv(x2d, mix_gain[None, :], w_qkv.astype(BF16), bsz, seq)
    att = _attention(folded, bsz, seq).reshape(bsz * seq, C_OUT)
    return _out_ffn(x2d, [att], [w_out.astype(BF16)], ffn_gain[None, :], w_gu.astype(BF16),
                    ffn_cw, ffn_cb[None, :], w_down.astype(BF16), final_gain[None, :], bsz, seq, final)


def kernel(x, mix_norm, ffn_norm, ab_w_in, ab_i_bias, ab_f_bias, ab_head_norm, ab_conv_w, ab_conv_b, ab_ln_g, ab_ln_b, ab_w_out, c_w_qkv, c_w_out, ffn_w_gu, ffn_conv_w, ffn_conv_b, ffn_w_down, final_norm):
    bsz, seq, _ = x.shape
    depth = mix_norm.shape[0]
    x2d = x.reshape(bsz * seq, D_MODEL)
    for layer in range(depth):
        j = layer // 2
        final = layer == depth - 1
        ffn_args = (ffn_w_gu[layer], ffn_conv_w[layer], ffn_conv_b[layer], ffn_w_down[layer], final_norm, bsz, seq, final)
        if layer % 2 == 0:
            x2d = _layer0(x2d, mix_norm[layer], ffn_norm[layer], ab_w_in[j], ab_i_bias[j], ab_f_bias[j],
                          ab_head_norm[j], ab_conv_w[j], ab_conv_b[j], ab_ln_g[j], ab_ln_b[j], ab_w_out[j], *ffn_args)
        else:
            x2d = _layer1(x2d, mix_norm[layer], ffn_norm[layer], c_w_qkv[j], c_w_out[j], *ffn_args)
    return x2d.reshape(bsz, seq, D_MODEL)
```
